```python
import math
import jax, jax.numpy as jnp
from jax import lax
import numpy as np


D_MODEL = 1024
BATCH = 1
SEQ = 16384
DEPTH = 1
DEC_BATCH = 128
DEC_SEQ = 8
PAST_LEN = 8192
PAGE_SIZE = 128

FOX_HEADS = 8
FOX_HEAD_DIM = 64
FOX_WIDTH = FOX_HEADS * FOX_HEAD_DIM
DIFF_HEADS = 4
DIFF_HEAD_DIM = 64
DIFF_V_DIM = 2 * DIFF_HEAD_DIM
DIFF_QK_WIDTH = DIFF_HEADS * 2 * DIFF_HEAD_DIM
DIFF_WIDTH = DIFF_HEADS * DIFF_V_DIM
MIX_WIDTH = FOX_WIDTH + DIFF_WIDTH
FQ0 = 0
FK0 = FQ0 + FOX_WIDTH
FV0 = FK0 + FOX_WIDTH
FF0 = FV0 + FOX_WIDTH
DQ0 = FF0 + FOX_HEADS
DK0 = DQ0 + DIFF_QK_WIDTH
DV0 = DK0 + DIFF_QK_WIDTH
IN_WIDTH = DV0 + DIFF_WIDTH
N_EXPERTS = 32
TOP_K = 4
D_EXPERT = D_MODEL
SWIGLU_ALPHA = 1.702
SWIGLU_LIMIT = 7.0
N_BUCKETS = 32
MAX_DISTANCE = 128
Q_BLOCK = 128
RMS_EPS = 1e-6
NEG_INF = -1e30
FORGET_BIAS_INIT = 3.0

kernel_name = "hymba_fox_diffattn_moe_step"


def rms_norm(x, g):
    xf = x.astype(jnp.float32)
    y = xf * lax.rsqrt(jnp.mean(xf * xf, axis=-1, keepdims=True) + RMS_EPS)
    return (y * g.astype(jnp.float32)).astype(x.dtype)


def split_mixers(z, b_f):
    lead = z.shape[:-1]
    fq = z[..., FQ0:FK0].reshape(lead + (FOX_HEADS, FOX_HEAD_DIM))
    fk = z[..., FK0:FV0].reshape(lead + (FOX_HEADS, FOX_HEAD_DIM))
    fv = z[..., FV0:FF0].reshape(lead + (FOX_HEADS, FOX_HEAD_DIM))
    logf = jax.nn.log_sigmoid(z[..., FF0:DQ0].astype(jnp.float32) + b_f.astype(jnp.float32)).astype(z.dtype)
    dq = z[..., DQ0:DK0].reshape(lead + (DIFF_HEADS, 2, DIFF_HEAD_DIM))
    dk = z[..., DK0:DV0].reshape(lead + (DIFF_HEADS, 2 * DIFF_HEAD_DIM))
    dv = z[..., DV0:IN_WIDTH].reshape(lead + (DIFF_HEADS, DIFF_V_DIM))
    return fq, fk, fv, logf, dq, dk, dv


def t5_bucket(rel):
    n = jnp.maximum(rel, 0)
    max_exact = N_BUCKETS // 2
    nf = jnp.maximum(n, 1).astype(jnp.float32)
    large = max_exact + (jnp.log(nf / max_exact) / math.log(MAX_DISTANCE / max_exact)
                         * (N_BUCKETS - max_exact)).astype(jnp.int32)
    large = jnp.minimum(large, N_BUCKETS - 1)
    return jnp.where(n < max_exact, n, large)


def rel_bias_logits(rel_bias, qpos, kpos):
    b = t5_bucket(qpos[:, None] - kpos[None, :])
    return jnp.transpose(rel_bias[b].astype(jnp.float32), (2, 0, 1))


def fox_attend(q, k, v, cq, ck, mask):
    s = jnp.einsum('bqhd,bkhd->bhqk', q, k).astype(jnp.float32) * (FOX_HEAD_DIM ** -0.5)
    decay = jnp.swapaxes(cq, 1, 2).astype(jnp.float32)[..., :, None] - jnp.swapaxes(ck, 1, 2).astype(jnp.float32)[..., None, :]
    s = jnp.where(mask, s + decay, NEG_INF)
    p = jax.nn.softmax(s, axis=-1)
    return jnp.einsum('bhqk,bkhd->bqhd', p.astype(v.dtype), v)


def diff_attend(q, k, v, bias, mask, lam):
    k = k.reshape(k.shape[:-1] + (2, DIFF_HEAD_DIM))
    s = jnp.einsum('bqhid,bkhid->bhiqk', q, k).astype(jnp.float32) * (DIFF_HEAD_DIM ** -0.5)
    s = jnp.where(mask, s + bias[None, :, None], NEG_INF)
    p = jax.nn.softmax(s, axis=-1)
    a = p[:, :, 0] - lam * p[:, :, 1]
    return jnp.einsum('bhqk,bkhe->bqhe', a.astype(v.dtype), v)


def moe(h, w_r, b_r, w_gu, b_gu, w_d, b_d):
    logits = (h @ w_r + b_r).astype(jnp.float32)
    top_v, top_i = lax.top_k(logits, TOP_K)
    gates = jax.nn.softmax(top_v, axis=-1)
    dense_gate = jnp.sum(jax.nn.one_hot(top_i, N_EXPERTS, dtype=jnp.float32) * gates[..., None], axis=1)
    out = jnp.zeros(h.shape, jnp.float32)
    for e in range(N_EXPERTS):
        gu = h @ w_gu[e] + b_gu[e]
        g = jnp.minimum(gu[..., :D_EXPERT], SWIGLU_LIMIT)
        u = jnp.clip(gu[..., D_EXPERT:], -SWIGLU_LIMIT, SWIGLU_LIMIT)
        act = (u + 1) * (g * jax.nn.sigmoid(SWIGLU_ALPHA * g))
        y = act @ w_d[e] + b_d[e]
        out = out + dense_gate[:, e:e + 1] * y.astype(jnp.float32)
    return out.astype(h.dtype)


def setup_inputs(seed: int = 0) -> dict:
    key = jax.random.key(seed)
    ks = jax.random.split(key, 32)
    f32 = jnp.float32
    n_pages = PAST_LEN // PAGE_SIZE
    n_pool = (DEC_BATCH * n_pages * 5) // 4

    def nrm(k, shape, s):
        return s * jax.random.normal(k, shape, f32)

    x_prompt = nrm(ks[0], (BATCH, SEQ, D_MODEL), 1.0)
    x_sample = nrm(ks[1], (DEC_BATCH, DEC_SEQ, D_MODEL), 1.0)
    cache_fox_k = nrm(ks[2], (DEPTH, n_pool, PAGE_SIZE, FOX_HEADS, FOX_HEAD_DIM), 1.0)
    cache_fox_v = nrm(ks[3], (DEPTH, n_pool, PAGE_SIZE, FOX_HEADS, FOX_HEAD_DIM), 1.0)
    cache_fox_logf = jax.nn.log_sigmoid(FORGET_BIAS_INIT + jax.random.normal(ks[4], (DEPTH, n_pool, PAGE_SIZE, FOX_HEADS), f32))
    cache_diff_k = nrm(ks[5], (DEPTH, n_pool, PAGE_SIZE, DIFF_HEADS, 2 * DIFF_HEAD_DIM), 1.0)
    cache_diff_v = nrm(ks[6], (DEPTH, n_pool, PAGE_SIZE, DIFF_HEADS, DIFF_V_DIM), 1.0)
    page_table = jax.random.permutation(ks[7], n_pool)[: DEC_BATCH * n_pages].reshape(DEC_BATCH, n_pages).astype(jnp.int32)

    rel_bias = nrm(ks[8], (N_BUCKETS, DIFF_HEADS), 0.5)
    norm1_g = 1.0 + nrm(ks[9], (DEPTH, D_MODEL), 0.02)
    w_in = nrm(ks[10], (DEPTH, D_MODEL, IN_WIDTH), D_MODEL ** -0.5)
    b_fgate = FORGET_BIAS_INIT + nrm(ks[11], (DEPTH, FOX_HEADS), 0.5)
    lambda_q1 = nrm(ks[12], (DEPTH, DIFF_HEAD_DIM), 0.1)
    lambda_k1 = nrm(ks[13], (DEPTH, DIFF_HEAD_DIM), 0.1)
    lambda_q2 = nrm(ks[14], (DEPTH, DIFF_HEAD_DIM), 0.1)
    lambda_k2 = nrm(ks[15], (DEPTH, DIFF_HEAD_DIM), 0.1)
    subln_g = 1.0 + nrm(ks[16], (DEPTH, DIFF_V_DIM), 0.02)
    w_out = nrm(ks[17], (DEPTH, MIX_WIDTH, D_MODEL), MIX_WIDTH ** -0.5)
    norm2_g = 1.0 + nrm(ks[18], (DEPTH, D_MODEL), 0.02)
    w_router = nrm(ks[19], (DEPTH, D_MODEL, N_EXPERTS), D_MODEL ** -0.5)
    b_router = nrm(ks[20], (DEPTH, N_EXPERTS), 0.01)
    w_gate_up = nrm(ks[21], (DEPTH, N_EXPERTS, D_MODEL, 2 * D_EXPERT), D_MODEL ** -0.5)
    b_gate_up = nrm(ks[22], (DEPTH, N_EXPERTS, 2 * D_EXPERT), 0.01)
    w_down = nrm(ks[23], (DEPTH, N_EXPERTS, D_EXPERT, D_MODEL), D_EXPERT ** -0.5)
    b_down = nrm(ks[24], (DEPTH, N_EXPERTS, D_MODEL), 0.01)
    final_norm_g = 1.0 + nrm(ks[25], (D_MODEL,), 0.02)
    return {
        "x_prompt": x_prompt, "x_sample": x_sample,
        "cache_fox_k": cache_fox_k, "cache_fox_v": cache_fox_v, "cache_fox_logf": cache_fox_logf,
        "cache_diff_k": cache_diff_k, "cache_diff_v": cache_diff_v, "page_table": page_table,
        "rel_bias": rel_bias, "norm1_g": norm1_g, "w_in": w_in, "b_fgate": b_fgate,
        "lambda_q1": lambda_q1, "lambda_k1": lambda_k1, "lambda_q2": lambda_q2, "lambda_k2": lambda_k2,
        "subln_g": subln_g, "w_out": w_out, "norm2_g": norm2_g,
        "w_router": w_router, "b_router": b_router, "w_gate_up": w_gate_up, "b_gate_up": b_gate_up,
        "w_down": w_down, "b_down": b_down, "final_norm_g": final_norm_g,
    }


def reference(x_prompt, x_sample, cache_fox_k, cache_fox_v, cache_fox_logf, cache_diff_k, cache_diff_v,
              page_table, rel_bias, norm1_g, w_in, b_fgate, lambda_q1, lambda_k1, lambda_q2, lambda_k2,
              subln_g, w_out, norm2_g, w_router, b_router, w_gate_up, b_gate_up, w_down, b_down,
              final_norm_g):
    f32 = jnp.float32
    B, S, _ = x_prompt.shape
    DB, T, _ = x_sample.shape
    P = page_table.shape[1] * PAGE_SIZE
    n_blocks = S // Q_BLOCK
    kpos_p = jnp.arange(S)
    qpos_s = P + jnp.arange(T)
    kpos_s = jnp.arange(P + T)
    mask_s = kpos_s[None, :] <= qpos_s[:, None]
    bias_s = rel_bias_logits(rel_bias, qpos_s, kpos_s)

    hp, hs = x_prompt, x_sample
    fk_p_l, fv_p_l, lf_p_l, dk_p_l, dv_p_l = [], [], [], [], []
    fk_s_l, fv_s_l, lf_s_l, dk_s_l, dv_s_l = [], [], [], [], []
    for l in range(DEPTH):
        lam_init = 0.8 - 0.6 * math.exp(-0.3 * l)
        lam = (jnp.exp(jnp.sum(lambda_q1[l].astype(f32) * lambda_k1[l].astype(f32)))
               - jnp.exp(jnp.sum(lambda_q2[l].astype(f32) * lambda_k2[l].astype(f32))) + lam_init)

        zp = rms_norm(hp, norm1_g[l]) @ w_in[l]
        fq, fk, fv, lf, dq, dk, dv = split_mixers(zp, b_fgate[l])
        cf = jnp.cumsum(lf.astype(f32), axis=1)

        def prompt_block(i):
            q0 = i * Q_BLOCK
            qpos = q0 + jnp.arange(Q_BLOCK)
            mask = kpos_p[None, :] <= qpos[:, None]
            fo1 = fox_attend(lax.dynamic_slice_in_dim(fq, q0, Q_BLOCK, 1), fk, fv,
                             lax.dynamic_slice_in_dim(cf, q0, Q_BLOCK, 1), cf, mask)
            do1 = diff_attend(lax.dynamic_slice_in_dim(dq, q0, Q_BLOCK, 1), dk, dv,
                              rel_bias_logits(rel_bias, qpos, kpos_p), mask, lam)
            return fo1, do1

        fo_b, do_b = lax.map(prompt_block, jnp.arange(n_blocks))
        fo_p = jnp.moveaxis(fo_b, 0, 1).reshape(B, S, FOX_WIDTH)
        do_p = jnp.moveaxis(do_b, 0, 1).reshape(B, S, DIFF_HEADS, DIFF_V_DIM)
        do_p = rms_norm(do_p, subln_g[l]) * (1.0 - lam_init)
        hp = hp + jnp.concatenate([fo_p, do_p.reshape(B, S, DIFF_WIDTH)], axis=-1) @ w_out[l]

        zs = rms_norm(hs, norm1_g[l]) @ w_in[l]
        sq, sk, sv, slf, sdq, sdk, sdv = split_mixers(zs, b_fgate[l])

        def sample_seq(args):
            pages, q1, k1, v1, lf1, dq1, dk1, dv1 = args

            def past(pool):
                g = pool[l, pages]
                return g.reshape((P,) + g.shape[2:])

            kf = jnp.concatenate([past(cache_fox_k), k1], axis=0)
            vf = jnp.concatenate([past(cache_fox_v), v1], axis=0)
            c = jnp.cumsum(jnp.concatenate([past(cache_fox_logf).astype(f32), lf1.astype(f32)], axis=0), axis=0)
            fo1 = fox_attend(q1[None], kf[None], vf[None], c[None, P:], c[None], mask_s)[0]
            kd = jnp.concatenate([past(cache_diff_k), dk1], axis=0)
            vd = jnp.concatenate([past(cache_diff_v), dv1], axis=0)
            do1 = diff_attend(dq1[None], kd[None], vd[None], bias_s, mask_s, lam)[0]
            return fo1, do1

        fo_s, do_s = lax.map(sample_seq, (page_table, sq, sk, sv, slf, sdq, sdk, sdv))
        fo_s = fo_s.reshape(DB, T, FOX_WIDTH)
        do_s = rms_norm(do_s, subln_g[l]) * (1.0 - lam_init)
        hs = hs + jnp.concatenate([fo_s, do_s.reshape(DB, T, DIFF_WIDTH)], axis=-1) @ w_out[l]

        tok = jnp.concatenate([hp.reshape(B * S, D_MODEL), hs.reshape(DB * T, D_MODEL)], axis=0)
        ff = moe(rms_norm(tok, norm2_g[l]), w_router[l], b_router[l], w_gate_up[l], b_gate_up[l],
                 w_down[l], b_down[l])
        hp = hp + ff[: B * S].reshape(B, S, D_MODEL)
        hs = hs + ff[B * S:].reshape(DB, T, D_MODEL)

        fk_p_l.append(fk); fv_p_l.append(fv); lf_p_l.append(lf); dk_p_l.append(dk); dv_p_l.append(dv)
        fk_s_l.append(sk); fv_s_l.append(sv); lf_s_l.append(slf); dk_s_l.append(sdk); dv_s_l.append(sdv)

    y_prompt = rms_norm(hp, final_norm_g)
    y_sample = rms_norm(hs, final_norm_g)
    return (y_prompt, y_sample,
            jnp.stack(fk_p_l), jnp.stack(fv_p_l), jnp.stack(lf_p_l), jnp.stack(dk_p_l), jnp.stack(dv_p_l),
            jnp.stack(fk_s_l), jnp.stack(fv_s_l), jnp.stack(lf_s_l), jnp.stack(dk_s_l), jnp.stack(dv_s_l))
```

```python
import functools
import math

import numpy as np
import jax
import jax.numpy as jnp
from jax import lax
from jax.experimental import pallas as pl
from jax.experimental.pallas import tpu as pltpu

F32 = jnp.float32
BF16 = jnp.bfloat16

D_MODEL = 1024
FOX_HEADS = 8
FOX_HEAD_DIM = 64
DIFF_HEADS = 4
DIFF_HEAD_DIM = 64
DIFF_V_DIM = 128
FOX_WIDTH = 512
DIFF_WIDTH = 512
N_EXPERTS = 32
TOP_K = 4
D_EXPERT = 1024
SWIGLU_ALPHA = 1.702
SWIGLU_LIMIT = 7.0
N_BUCKETS = 32
MAX_DISTANCE = 128
PAGE_SIZE = 128
RMS_EPS = 1e-6
NEG_INF = -1e30
LANES = 128
SLOT = 128
QK_SCALE = FOX_HEAD_DIM ** -0.5
VMEM_LIMIT = 56 * 1024 * 1024

FQ0, FK0, FV0, FF0 = 0, 512, 1024, 1536
DQ0, DK0, DV0, IN_WIDTH = 1544, 2056, 2568, 3080


def _split3(x):
    hi = x.astype(BF16)
    r = x - hi.astype(F32)
    mid = r.astype(BF16)
    lo = (r - mid.astype(F32)).astype(BF16)
    return hi, mid, lo


def _rms(x, g):
    return x * lax.rsqrt(jnp.mean(x * x, axis=-1, keepdims=True) + RMS_EPS) * g


def _log_sigmoid(x):
    return jnp.minimum(x, 0.0) - jnp.log1p(jnp.exp(-jnp.abs(x)))


def _nt_dot(a, b):
    return lax.dot_general(a, b, (((1,), (1,)), ((), ())), preferred_element_type=F32)


def _t5_bucket_np(n):
    n = np.maximum(n, 0)
    max_exact = N_BUCKETS // 2
    nf = np.maximum(n, 1).astype(np.float64)
    large = max_exact + (np.log(nf / max_exact) / math.log(MAX_DISTANCE / max_exact)
                         * (N_BUCKETS - max_exact)).astype(np.int64)
    large = np.minimum(large, N_BUCKETS - 1)
    return np.where(n < max_exact, n, large)


def _proj_kernel(x_ref, g_ref, w_ref, bf_ref, tri_ref,
                 fq_ref, fk_ref, fv_ref, dq_ref, dk_ref, dv_ref, lf_ref, lfp_ref, cf_ref,
                 carry_ref, *, with_cumsum, tm):
    i = pl.program_id(0)
    xn = _rms(x_ref[...], g_ref[...]).astype(BF16)
    z = jnp.dot(xn, w_ref[...], preferred_element_type=F32)
    for k, ref in enumerate((fq_ref, fk_ref, fv_ref, dq_ref, dk_ref, dv_ref)):
        ref[...] = z[:, k * 512:(k + 1) * 512]
    lf = _log_sigmoid(z[:, 3072:3072 + LANES] + bf_ref[...])
    lane = lax.broadcasted_iota(jnp.int32, (tm, LANES), 1)
    lf = jnp.where(lane < FOX_HEADS, lf, 0.0)
    lf_ref[...] = lf[:, :FOX_HEADS]
    lfp_ref[...] = lf
    if with_cumsum:
        @pl.when(i == 0)
        def _():
            carry_ref[...] = jnp.zeros_like(carry_ref)
        hi, mid, lo = _split3(lf)
        tri = tri_ref[...]
        cum = (jnp.dot(tri, hi, preferred_element_type=F32)
               + jnp.dot(tri, mid, preferred_element_type=F32)
               + jnp.dot(tri, lo, preferred_element_type=F32))
        cf = cum + carry_ref[0:1, :]
        cf_ref[...] = cf
        carry_ref[0:1, :] = cf[tm - 1:tm, :]
    else:
        cf_ref[...] = jnp.zeros_like(cf_ref)


def _proj(x, g, w_c, bf_pad, with_cumsum, tm):
    n = x.shape[0]
    assert n % tm == 0
    tri = jnp.asarray(np.tril(np.ones((tm, tm), np.float32)), BF16)
    wide = jax.ShapeDtypeStruct((n, 512), F32)
    row = lambda i: (i, 0)
    const = lambda i: (0, 0)
    return pl.pallas_call(
        functools.partial(_proj_kernel, with_cumsum=with_cumsum, tm=tm),
        grid=(n // tm,),
        in_specs=[pl.BlockSpec((tm, D_MODEL), row), pl.BlockSpec((1, D_MODEL), const),
                  pl.BlockSpec(w_c.shape, const), pl.BlockSpec((1, LANES), const),
                  pl.BlockSpec((tm, tm), const)],
        out_specs=[pl.BlockSpec((tm, 512), row)] * 6
        + [pl.BlockSpec((tm, FOX_HEADS), row), pl.BlockSpec((tm, LANES), row), pl.BlockSpec((tm, LANES), row)],
        out_shape=[wide] * 6 + [jax.ShapeDtypeStruct((n, FOX_HEADS), F32),
                                jax.ShapeDtypeStruct((n, LANES), F32), jax.ShapeDtypeStruct((n, LANES), F32)],
        scratch_shapes=[pltpu.VMEM((8, LANES), F32)],
        compiler_params=pltpu.CompilerParams(dimension_semantics=("arbitrary",), vmem_limit_bytes=VMEM_LIMIT),
        name="proj",
    )(x, g, w_c, bf_pad, tri)


def _slots_kernel(x_ref, g_ref, cf_ref, w_ref, place_ref, const_ref, scale_ref, o_ref, xn_ref, cs_ref):
    @pl.when(pl.program_id(1) == 0)
    def _():
        xn_ref[...] = _rms(x_ref[...], g_ref[...]).astype(BF16)
        hi, mid, lo = _split3(cf_ref[...])
        cs_ref[:, 0:LANES] = hi
        cs_ref[:, LANES:2 * LANES] = mid
        cs_ref[:, 2 * LANES:3 * LANES] = lo
    z = jnp.dot(xn_ref[...], w_ref[0], preferred_element_type=F32)
    aug = jnp.dot(cs_ref[...], place_ref[0], preferred_element_type=F32)
    zz = z * scale_ref[0] + aug + const_ref[0]
    for h in range(8):
        o_ref[0, h] = zz[:, h * SLOT:(h + 1) * SLOT].astype(BF16)


def _slot_constants():
    place = np.zeros((6, 3 * LANES, 8 * SLOT), np.float32)
    const = np.zeros((6, 1, 8 * SLOT), np.float32)
    scale = np.ones((6, 1, 8 * SLOT), np.float32)
    for h in range(FOX_HEADS):
        for p in range(3):
            place[0, p * LANES + h, h * SLOT + 64 + p] = 1.0
            const[0, 0, h * SLOT + 67 + p] = 1.0
            const[1, 0, h * SLOT + 64 + p] = 1.0
            place[1, p * LANES + h, h * SLOT + 67 + p] = -1.0
        const[2, 0, h * SLOT + 64] = 1.0
    scale[0] = QK_SCALE
    scale[3] = QK_SCALE
    for h in range(DIFF_HEADS):
        const[5, 0, (2 * h + 1) * SLOT] = 1.0
    return place, const, scale


def _slots(x, g, cf, w_slots, tm):
    s = x.shape[0]
    place, const, scale = _slot_constants()
    return pl.pallas_call(
        _slots_kernel,
        grid=(s // tm, 6),
        in_specs=[pl.BlockSpec((tm, D_MODEL), lambda i, j: (i, 0)), pl.BlockSpec((1, D_MODEL), lambda i, j: (0, 0)),
                  pl.BlockSpec((tm, LANES), lambda i, j: (i, 0)),
                  pl.BlockSpec((1, D_MODEL, 8 * SLOT), lambda i, j: (j, 0, 0)),
                  pl.BlockSpec((1, 3 * LANES, 8 * SLOT), lambda i, j: (j, 0, 0)),
                  pl.BlockSpec((1, 1, 8 * SLOT), lambda i, j: (j, 0, 0)),
                  pl.BlockSpec((1, 1, 8 * SLOT), lambda i, j: (j, 0, 0))],
        out_specs=pl.BlockSpec((1, 8, tm, SLOT), lambda i, j: (j, 0, i, 0)),
        out_shape=jax.ShapeDtypeStruct((6, 8, s, SLOT), BF16),
        scratch_shapes=[pltpu.VMEM((tm, D_MODEL), BF16), pltpu.VMEM((tm, 3 * LANES), BF16)],
        compiler_params=pltpu.CompilerParams(dimension_semantics=("arbitrary", "arbitrary"),
                                             vmem_limit_bytes=VMEM_LIMIT),
        name="slots",
    )(x, g, cf, w_slots, jnp.asarray(place, BF16), jnp.asarray(const), jnp.asarray(scale))


def _lambda(lam_ref, lam_init):
    lq1, lk1, lq2, lk2 = (lam_ref[k:k + 1, :] for k in range(4))
    return (jnp.exp(jnp.sum(lq1 * lk1, axis=1, keepdims=True))
            - jnp.exp(jnp.sum(lq2 * lk2, axis=1, keepdims=True)) + lam_init)


def _attn_kernel(qi_ref, ki_ref, fq_ref, fk_ref, fv_ref, dq_ref, dk_ref, dv_ref, bias_ref, lam_ref, subg_ref,
                 o_ref, m_ref, accf_ref, accd_ref, accl_ref, *, tq, lam_init):
    step = pl.program_id(0)
    qb = qi_ref[step]
    kb = ki_ref[step]

    @pl.when(kb == 0)
    def _():
        m_ref[...] = jnp.full_like(m_ref, NEG_INF)
        accf_ref[...] = jnp.zeros_like(accf_ref)
        accd_ref[...] = jnp.zeros_like(accd_ref)
        accl_ref[...] = jnp.zeros_like(accl_ref)

    def online(s, midx):
        m_prev = m_ref[midx]
        m_new = jnp.maximum(m_prev, jnp.max(s, axis=1, keepdims=True))
        m_ref[midx] = m_new
        return jnp.exp(m_prev - m_new), jnp.exp(s - m_new).astype(BF16)

    def process(kind):
        def fox_head(h, carry):
            s = _nt_dot(fq_ref[0, h], fk_ref[0, h])
            if kind == 0:
                row = lax.broadcasted_iota(jnp.int32, s.shape, 0)
                col = lax.broadcasted_iota(jnp.int32, s.shape, 1)
                s = jnp.where(row >= col, s, NEG_INF)
            alpha, p = online(s, h)
            accf_ref[h] = alpha * accf_ref[h] + jnp.dot(p, fv_ref[0, h], preferred_element_type=F32)
            return carry

        lax.fori_loop(0, FOX_HEADS, fox_head, 0)

        def diff_map(slot, carry):
            hd = slot // 2
            s = _nt_dot(dq_ref[0, slot], dk_ref[0, slot])
            if kind < 2:
                s = s + bias_ref[kind, hd]
            alpha, p = online(s, FOX_HEADS + slot)
            accd_ref[slot] = alpha * accd_ref[slot] + jnp.dot(p, dv_ref[0, 2 * hd], preferred_element_type=F32)
            accl_ref[slot] = alpha * accl_ref[slot] + jnp.dot(p, dv_ref[0, 2 * hd + 1], preferred_element_type=F32)
            return carry

        lax.fori_loop(0, 2 * DIFF_HEADS, diff_map, 0)

    @pl.when(kb < qb - 1)
    def _():
        process(2)

    @pl.when(kb == qb - 1)
    def _():
        process(1)

    @pl.when(kb == qb)
    def _():
        process(0)
        for h in range(FOX_HEADS):
            a = accf_ref[h]
            o_ref[:, h * SLOT:(h + 1) * SLOT] = (a / a[:, FOX_HEAD_DIM:FOX_HEAD_DIM + 1]).astype(BF16)
        lam = _lambda(lam_ref, lam_init)
        for hd in range(DIFF_HEADS):
            o1 = accd_ref[2 * hd] / accl_ref[2 * hd][:, 0:1]
            o2 = accd_ref[2 * hd + 1] / accl_ref[2 * hd + 1][:, 0:1]
            d = _rms(o1 - lam * o2, subg_ref[...]) * (1.0 - lam_init)
            o_ref[:, (FOX_HEADS + hd) * SLOT:(FOX_HEADS + hd + 1) * SLOT] = d.astype(BF16)


def _attn(ops, bias, lam_vecs, subg, tq, lam_init):
    s = ops.shape[2]
    nq = s // tq
    qi = np.concatenate([np.full(q + 1, q, np.int32) for q in range(nq)])
    ki = np.concatenate([np.arange(q + 1, dtype=np.int32) for q in range(nq)])

    def opspec(j, use_q):
        if use_q:
            return pl.BlockSpec((1, 8, tq, SLOT), lambda p, qi_r, ki_r: (j, 0, qi_r[p], 0))
        return pl.BlockSpec((1, 8, tq, SLOT), lambda p, qi_r, ki_r: (j, 0, ki_r[p], 0))

    grid_spec = pltpu.PrefetchScalarGridSpec(
        num_scalar_prefetch=2,
        grid=(len(qi),),
        in_specs=[opspec(0, True), opspec(1, False), opspec(2, False), opspec(3, True), opspec(4, False),
                  opspec(5, False),
                  pl.BlockSpec(bias.shape, lambda p, qi_r, ki_r: (0, 0, 0, 0)),
                  pl.BlockSpec((4, DIFF_HEAD_DIM), lambda p, qi_r, ki_r: (0, 0)),
                  pl.BlockSpec((1, DIFF_V_DIM), lambda p, qi_r, ki_r: (0, 0))],
        out_specs=pl.BlockSpec((tq, 12 * SLOT), lambda p, qi_r, ki_r: (qi_r[p], 0)),
        scratch_shapes=[pltpu.VMEM((16, tq, 1), F32), pltpu.VMEM((8, tq, SLOT), F32),
                        pltpu.VMEM((8, tq, SLOT), F32), pltpu.VMEM((8, tq, SLOT), F32)],
    )
    return pl.pallas_call(
        functools.partial(_attn_kernel, tq=tq, lam_init=lam_init),
        grid_spec=grid_spec,
        out_shape=jax.ShapeDtypeStruct((s, 12 * SLOT), BF16),
        compiler_params=pltpu.CompilerParams(dimension_semantics=("arbitrary",), vmem_limit_bytes=VMEM_LIMIT),
        name="attn",
    )(jnp.asarray(qi), jnp.asarray(ki), ops, ops, ops, ops, ops, ops, bias, lam_vecs, subg)


def _prompt_bias(rel_bias, tq):
    d = np.arange(2 * tq)
    table = rel_bias[_t5_bucket_np(d)] - rel_bias[N_BUCKETS - 1][None, :]
    i = np.arange(tq)[:, None]
    j = np.arange(tq)[None, :]
    diag = jnp.where(jnp.asarray(i >= j)[..., None], table[np.maximum(i - j, 0)], NEG_INF)
    sub = table[tq + i - j]
    return jnp.transpose(jnp.stack([diag, sub]), (0, 3, 1, 2)).astype(F32)


def _sample_kernel(pt_ref,
                   fq_ref, fkn_ref, fvn_ref, lfn_ref, dq_ref, dkn_ref, dvn_ref,
                   blast_ref, bnew_ref, hmask_ref, us_ref, e3_ref, e1_ref, lam_ref, subg_ref,
                   fkc_ref, fvc_ref, dkc_ref, dvc_ref, lfc_ref,
                   o_ref,
                   kbuf, lbuf, sem, qbf_ref, qbd_ref, ml_ref, accf_ref, accd_ref, carry_ref,
                   *, n_pages, ch, nch, t_new, lam_init):
    step = pl.program_id(0)
    nsteps = pl.num_programs(0)
    c = step % nch
    slot = step % 2
    ckeys = ch * PAGE_SIZE

    def copies(st, sl):
        b = st // nch
        base = b * n_pages + n_pages - (st % nch + 1) * ch
        out = []
        for p in range(ch):
            pg = pt_ref[base + p]
            for a, cache in enumerate((fkc_ref, fvc_ref, dkc_ref, dvc_ref)):
                out.append(pltpu.make_async_copy(cache.at[pg], kbuf.at[sl, a, pl.ds(p * PAGE_SIZE, PAGE_SIZE)],
                                                 sem.at[sl, a]))
            out.append(pltpu.make_async_copy(lfc_ref.at[pg], lbuf.at[sl, pl.ds(p * 8, 8)], sem.at[sl, 4]))
        return out

    @pl.when(step == 0)
    def _():
        for cp in copies(step, slot):
            cp.start()

    @pl.when(step + 1 < nsteps)
    def _():
        for cp in copies(step + 1, 1 - slot):
            cp.start()

    def col_of(rowvec):
        return jnp.transpose(jnp.broadcast_to(rowvec, (LANES, LANES)))[:, 0:1]

    def expand(x8):
        hi, mid, lo = _split3(x8)
        e1 = e1_ref[...]
        return (jnp.dot(hi, e1, preferred_element_type=F32) + jnp.dot(mid, e1, preferred_element_type=F32)
                + jnp.dot(lo, e1, preferred_element_type=F32))

    @pl.when(c == 0)
    def _():
        hmask = hmask_ref[...]
        zero = jnp.zeros((64, 512), F32)
        fq8 = fq_ref[0] * QK_SCALE
        dq8 = dq_ref[0] * QK_SCALE
        qbf_ref[...] = jnp.concatenate([jnp.concatenate([fq8] * 8, axis=0) * hmask, zero], axis=0)
        qbd_ref[...] = jnp.concatenate([zero, jnp.concatenate([dq8] * 8, axis=0) * hmask], axis=0)
        s = _nt_dot(fkn_ref[0], qbf_ref[...]) + _nt_dot(dkn_ref[0], qbd_ref[...])
        x = lfn_ref[0]
        row = lax.broadcasted_iota(jnp.int32, x.shape, 0)
        for sh in (1, 2, 4):
            x = x + jnp.where(row >= sh, pltpu.roll(x, sh, 0), 0.0)
        s = s - expand(x) + bnew_ref[...]
        m = jnp.max(s, axis=0, keepdims=True)
        p = jnp.exp(s - m)
        ml_ref[0:1, :] = m
        ml_ref[1:2, :] = jnp.sum(p, axis=0, keepdims=True)
        pad = jnp.zeros((LANES - t_new, LANES), F32)
        pt = jnp.transpose(jnp.concatenate([p, pad], axis=0))
        vpad = jnp.zeros((LANES - t_new, 512), F32)
        accf_ref[...] = jnp.dot(pt, jnp.concatenate([fvn_ref[0], vpad], axis=0), preferred_element_type=F32)
        accd_ref[...] = jnp.dot(pt, jnp.concatenate([dvn_ref[0], vpad], axis=0), preferred_element_type=F32)
        carry_ref[...] = jnp.zeros_like(carry_ref)

    for cp in copies(step, slot):
        cp.wait()

    x = lbuf[slot]
    hi, mid, lo = _split3(x)
    us = us_ref[...]
    within = (jnp.dot(hi, us, preferred_element_type=F32) + jnp.dot(mid, us, preferred_element_type=F32)
              + jnp.dot(lo, us, preferred_element_type=F32))
    tot = jnp.sum(x, axis=1, keepdims=True)
    run = carry_ref[:, 0:1]
    pieces = []
    zpad = jnp.zeros((LANES - 24, LANES), F32)
    for p in reversed(range(ch)):
        suf = within[p * 8:(p + 1) * 8] + run
        run = run + tot[p * 8:(p + 1) * 8]
        h3, m3, l3 = _split3(suf)
        stack = jnp.concatenate([h3.astype(F32), m3.astype(F32), l3.astype(F32), zpad], axis=0)
        pieces.append(jnp.transpose(stack))
    carry_ref[...] = jnp.broadcast_to(run, carry_ref.shape)
    dexp = jnp.dot(jnp.concatenate(pieces[::-1], axis=0).astype(BF16), e3_ref[...],
                   preferred_element_type=F32)

    s = _nt_dot(kbuf[slot, 0], qbf_ref[...]) + _nt_dot(kbuf[slot, 2], qbd_ref[...]) + dexp
    tail = s[ckeys - PAGE_SIZE:] + blast_ref[...] * jnp.where(c == 0, 1.0, 0.0)
    s = jnp.concatenate([s[:ckeys - PAGE_SIZE], tail], axis=0)

    m_prev = ml_ref[0:1, :]
    m_new = jnp.maximum(m_prev, jnp.max(s, axis=0, keepdims=True))
    alpha = jnp.exp(m_prev - m_new)
    p = jnp.exp(s - m_new)
    ml_ref[0:1, :] = m_new
    ml_ref[1:2, :] = alpha * ml_ref[1:2, :] + jnp.sum(p, axis=0, keepdims=True)
    pt = jnp.concatenate([jnp.transpose(p[k * LANES:(k + 1) * LANES]) for k in range(ckeys // LANES)], axis=1)
    acol = col_of(alpha)
    accf_ref[...] = acol * accf_ref[...] + jnp.dot(pt, kbuf[slot, 1], preferred_element_type=F32)
    accd_ref[...] = acol * accd_ref[...] + jnp.dot(pt, kbuf[slot, 3], preferred_element_type=F32)

    @pl.when(c == nch - 1)
    def _():
        lcol = col_of(ml_ref[1:2, :])
        hmask = hmask_ref[...]
        af = accf_ref[0:64, :] / lcol[0:64]
        fox = af[0:8] * hmask[0:8]
        for h in range(1, FOX_HEADS):
            fox = fox + af[h * 8:(h + 1) * 8] * hmask[h * 8:(h + 1) * 8]
        o_ref[0, :, 0:FOX_WIDTH] = fox.astype(BF16)
        lam = _lambda(lam_ref, lam_init)
        ad = accd_ref[64:128, :] / lcol[64:128]
        for hd in range(DIFF_HEADS):
            o1 = ad[(2 * hd) * 8:(2 * hd + 1) * 8, hd * DIFF_V_DIM:(hd + 1) * DIFF_V_DIM]
            o2 = ad[(2 * hd + 1) * 8:(2 * hd + 2) * 8, hd * DIFF_V_DIM:(hd + 1) * DIFF_V_DIM]
            d = _rms(o1 - lam * o2, subg_ref[...]) * (1.0 - lam_init)
            o_ref[0, :, FOX_WIDTH + hd * DIFF_V_DIM:FOX_WIDTH + (hd + 1) * DIFF_V_DIM] = d.astype(BF16)


def _sample_constants(t_new):
    hmask = np.zeros((64, 512), np.float32)
    for r in range(64):
        hmask[r, (r // 8) * 64:(r // 8 + 1) * 64] = 1.0
    us = np.tril(np.ones((PAGE_SIZE, PAGE_SIZE), np.float32), -1)
    e1 = np.zeros((LANES, LANES), np.float32)
    e3 = np.zeros((LANES, LANES), np.float32)
    for h in range(FOX_HEADS):
        e1[h, h * 8:(h + 1) * 8] = 1.0
        for p in range(3):
            e3[p * 8 + h, h * 8:(h + 1) * 8] = 1.0
    return hmask, us, e1, e3


def _sample_bias(rel_bias, t_new):
    far = rel_bias[N_BUCKETS - 1]
    jj = np.arange(PAGE_SIZE)[:, None]
    t = np.arange(t_new)[None, :]
    last = rel_bias[_t5_bucket_np(t + PAGE_SIZE - jj)] - far
    tp = np.arange(t_new)[:, None]
    new = rel_bias[_t5_bucket_np(t - tp)] - far
    valid = jnp.asarray(tp <= t)

    def cols(tab):
        x = jnp.transpose(tab, (0, 2, 1))
        x = jnp.broadcast_to(x[:, :, None, :], (tab.shape[0], DIFF_HEADS, 2, t_new))
        return x.reshape(tab.shape[0], 64)

    blast = jnp.concatenate([jnp.zeros((PAGE_SIZE, 64), F32), cols(last)], axis=1)
    vmask = jnp.where(valid, 0.0, NEG_INF).astype(F32)
    fox_mask = jnp.broadcast_to(vmask[:, None, :], (t_new, FOX_HEADS, t_new)).reshape(t_new, 64)
    diff_new = jnp.where(valid[..., None], new, NEG_INF)
    bnew = jnp.concatenate([fox_mask, cols(diff_new)], axis=1)
    return blast.astype(F32), bnew.astype(F32)


def _sample(page_table, fq, fkn, fvn, lfn, dq, dkn, dvn, blast, bnew, lam_vecs, subg,
            fkc, fvc, dkc, dvc, lfc, ch, lam_init):
    db, t_new = fq.shape[0], fq.shape[1]
    n_pages = page_table.shape[1]
    assert n_pages % ch == 0 and t_new == 8
    nch = n_pages // ch
    hmask, us, e1, e3 = _sample_constants(t_new)
    seq = lambda w: pl.BlockSpec((1, t_new, w), lambda s, pt: (s // nch, 0, 0))
    full = lambda shape: pl.BlockSpec(shape, lambda s, pt: tuple(0 for _ in shape))
    hbm = pl.BlockSpec(memory_space=pl.ANY)
    grid_spec = pltpu.PrefetchScalarGridSpec(
        num_scalar_prefetch=1,
        grid=(db * nch,),
        in_specs=[seq(512), seq(512), seq(512), seq(LANES), seq(512), seq(512), seq(512),
                  full((PAGE_SIZE, LANES)), full((t_new, LANES)), full((64, 512)), full((LANES, LANES)),
                  full((LANES, LANES)), full((LANES, LANES)), full((4, DIFF_HEAD_DIM)), full((1, DIFF_V_DIM)),
                  hbm, hbm, hbm, hbm, hbm],
        out_specs=pl.BlockSpec((1, t_new, D_MODEL), lambda s, pt: (s // nch, 0, 0)),
        scratch_shapes=[pltpu.VMEM((2, 4, ch * PAGE_SIZE, 512), F32), pltpu.VMEM((2, ch * 8, LANES), F32),
                        pltpu.SemaphoreType.DMA((2, 5)),
                        pltpu.VMEM((LANES, 512), F32), pltpu.VMEM((LANES, 512), F32),
                        pltpu.VMEM((8, LANES), F32), pltpu.VMEM((LANES, 512), F32), pltpu.VMEM((LANES, 512), F32),
                        pltpu.VMEM((8, LANES), F32)],
    )
    return pl.pallas_call(
        functools.partial(_sample_kernel, n_pages=n_pages, ch=ch, nch=nch, t_new=t_new, lam_init=lam_init),
        grid_spec=grid_spec,
        out_shape=jax.ShapeDtypeStruct((db, t_new, D_MODEL), BF16),
        compiler_params=pltpu.CompilerParams(dimension_semantics=("arbitrary",), vmem_limit_bytes=VMEM_LIMIT),
        name="sample",
    )(page_table.reshape(-1), fq, fkn, fvn, lfn, dq, dkn, dvn, blast, bnew,
      jnp.asarray(hmask), jnp.asarray(us, BF16), jnp.asarray(e3, BF16), jnp.asarray(e1, BF16), lam_vecs, subg,
      fkc, fvc, dkc, dvc, lfc)


def _post_kernel(x_ref, mix_ref, wo_ref, g2_ref, wr_ref, br_ref, tri_ref,
                 h_ref, hn_ref, slotc_ref, gatec_ref, slotr_ref, cnt_ref, *, tm):
    h = x_ref[...] + jnp.dot(mix_ref[...], wo_ref[...], preferred_element_type=F32)
    h_ref[...] = h
    hn = _rms(h, g2_ref[...])
    hn_ref[...] = hn.astype(BF16)
    logits = jnp.dot(hn, wr_ref[...], preferred_element_type=F32, precision=lax.Precision.HIGHEST) + br_ref[...]
    lane = lax.broadcasted_iota(jnp.int32, (tm, LANES), 1).astype(F32)
    work = logits
    vals, sels = [], []
    for _ in range(TOP_K):
        mx = jnp.max(work, axis=1, keepdims=True)
        ix = jnp.min(jnp.where(work == mx, lane, float(LANES)), axis=1, keepdims=True)
        sel = lane == ix
        vals.append(mx)
        sels.append(sel)
        work = jnp.where(sel, -3.0e38, work)
    exps = [jnp.exp(v - vals[0]) for v in vals]
    den = exps[0] + exps[1] + exps[2] + exps[3]
    gate = jnp.zeros((tm, LANES), F32)
    cnt = jnp.zeros((tm, LANES), F32)
    for sel, ex in zip(sels, exps):
        gate = jnp.where(sel, ex / den, gate)
        cnt = jnp.where(sel, 1.0, cnt)
    rank = jnp.dot(tri_ref[...], cnt.astype(BF16), preferred_element_type=F32)
    slot = jnp.where(cnt > 0.0, rank, -1.0)
    slotc_ref[...] = slot
    gatec_ref[...] = gate
    slotr_ref[...] = jnp.transpose(slot)
    cnt_ref[0] = jnp.broadcast_to(jnp.sum(cnt, axis=0, keepdims=True), (8, LANES))


def _post(x, mix, w_o, g2, wr_pad, br_pad, tm):
    n = x.shape[0]
    assert n % tm == 0 and tm % LANES == 0
    tri = jnp.asarray(np.tril(np.ones((tm, tm), np.float32), -1), BF16)
    row = lambda i: (i, 0)
    const = lambda i: (0, 0)
    return pl.pallas_call(
        functools.partial(_post_kernel, tm=tm),
        grid=(n // tm,),
        in_specs=[pl.BlockSpec((tm, D_MODEL), row), pl.BlockSpec((tm, mix.shape[1]), row),
                  pl.BlockSpec(w_o.shape, const), pl.BlockSpec((1, D_MODEL), const),
                  pl.BlockSpec((D_MODEL, LANES), const), pl.BlockSpec((1, LANES), const),
                  pl.BlockSpec((tm, tm), const)],
        out_specs=[pl.BlockSpec((tm, D_MODEL), row), pl.BlockSpec((tm, D_MODEL), row),
                   pl.BlockSpec((tm, LANES), row), pl.BlockSpec((tm, LANES), row),
                   pl.BlockSpec((LANES, tm), lambda i: (0, i)), pl.BlockSpec((1, 8, LANES), lambda i: (i, 0, 0))],
        out_shape=[jax.ShapeDtypeStruct((n, D_MODEL), F32), jax.ShapeDtypeStruct((n, D_MODEL), BF16),
                   jax.ShapeDtypeStruct((n, LANES), F32), jax.ShapeDtypeStruct((n, LANES), F32),
                   jax.ShapeDtypeStruct((LANES, n), F32), jax.ShapeDtypeStruct((n // tm, 8, LANES), F32)],
        compiler_params=pltpu.CompilerParams(dimension_semantics=("arbitrary",), vmem_limit_bytes=VMEM_LIMIT),
        name="post",
    )(x, mix, w_o, g2, wr_pad, br_pad, tri)


def _moe_kernel(cnt_ref, hn_ref, h_ref, slotr_ref, slotc_ref, gatec_ref, wgu_ref, bgu_ref, wd_ref, bd_ref, gf_ref,
                y_ref, acc_ref, *, tm, cap):
    i = pl.program_id(0)
    e = pl.program_id(1)

    @pl.when(e == 0)
    def _():
        acc_ref[...] = jnp.zeros_like(acc_ref)

    n = cnt_ref[i * N_EXPERTS + e]
    srow = slotr_ref[pl.ds(e, 1), :]
    sel = lax.broadcasted_iota(jnp.int32, (tm, LANES), 1) == e
    scol = jnp.sum(jnp.where(sel, slotc_ref[...], 0.0), axis=1, keepdims=True)
    gcol = jnp.sum(jnp.where(sel, gatec_ref[...], 0.0), axis=1, keepdims=True)
    sub_iota = lax.broadcasted_iota(jnp.int32, (cap, tm), 0).astype(F32)
    lane_iota = lax.broadcasted_iota(jnp.int32, (tm, cap), 1).astype(F32)

    def chunk(j, carry):
        base = (j * cap).astype(F32)
        gather = jnp.where(sub_iota == srow - base, 1.0, 0.0).astype(BF16)
        xg = jnp.dot(gather, hn_ref[...], preferred_element_type=F32).astype(BF16)
        gu = jnp.dot(xg, wgu_ref[0], preferred_element_type=F32) + bgu_ref[0]
        g = jnp.minimum(gu[:, :D_EXPERT], SWIGLU_LIMIT)
        u = jnp.clip(gu[:, D_EXPERT:], -SWIGLU_LIMIT, SWIGLU_LIMIT)
        act = (u + 1.0) * (g * jax.nn.sigmoid(SWIGLU_ALPHA * g))
        y = jnp.dot(act.astype(BF16), wd_ref[0], preferred_element_type=F32) + bd_ref[0]
        scatter = jnp.where(lane_iota == scol - base, 1.0, 0.0).astype(BF16)
        acc_ref[...] += gcol * jnp.dot(scatter, y.astype(BF16), preferred_element_type=F32)
        return carry

    lax.fori_loop(0, (n + cap - 1) // cap, chunk, 0)

    @pl.when(e == N_EXPERTS - 1)
    def _():
        y_ref[...] = _rms(h_ref[...] + acc_ref[...], gf_ref[...])


def _moe(counts, hn, h, slotr, slotc, gatec, wgu, bgu, wd, bd, gfin, tm, cap):
    n = hn.shape[0]
    tok = lambda i, e, c: (i, 0)
    exp3 = lambda i, e, c: (e, 0, 0)
    grid_spec = pltpu.PrefetchScalarGridSpec(
        num_scalar_prefetch=1,
        grid=(n // tm, N_EXPERTS),
        in_specs=[pl.BlockSpec((tm, D_MODEL), tok), pl.BlockSpec((tm, D_MODEL), tok),
                  pl.BlockSpec((LANES, tm), lambda i, e, c: (0, i)),
                  pl.BlockSpec((tm, LANES), tok), pl.BlockSpec((tm, LANES), tok),
                  pl.BlockSpec((1, D_MODEL, 2 * D_EXPERT), exp3), pl.BlockSpec((1, 1, 2 * D_EXPERT), exp3),
                  pl.BlockSpec((1, D_EXPERT, D_MODEL), exp3), pl.BlockSpec((1, 1, D_MODEL), exp3),
                  pl.BlockSpec((1, D_MODEL), lambda i, e, c: (0, 0))],
        out_specs=pl.BlockSpec((tm, D_MODEL), tok),
        scratch_shapes=[pltpu.VMEM((tm, D_MODEL), F32)],
    )
    return pl.pallas_call(
        functools.partial(_moe_kernel, tm=tm, cap=cap),
        grid_spec=grid_spec,
        out_shape=jax.ShapeDtypeStruct((n, D_MODEL), F32),
        compiler_params=pltpu.CompilerParams(dimension_semantics=("arbitrary", "arbitrary"),
                                             vmem_limit_bytes=VMEM_LIMIT),
        name="moe",
    )(counts, hn, h, slotr, slotc, gatec, wgu, bgu, wd, bd, gfin)


def _pad_heads(w, n_heads, width, slot):
    w = w.reshape(w.shape[0], n_heads, width)
    return jnp.pad(w, ((0, 0), (0, 0), (0, slot - width))).reshape(w.shape[0], n_heads * slot)


def _tile(n, options):
    for t in options:
        if n % t == 0:
            return t
    raise ValueError(f"no tile in {options} divides {n}")


def kernel(x_prompt, x_sample, cache_fox_k, cache_fox_v, cache_fox_logf, cache_diff_k, cache_diff_v, page_table,
           rel_bias, norm1_g, w_in, b_fgate, lambda_q1, lambda_k1, lambda_q2, lambda_k2, subln_g, w_out, norm2_g,
           w_router, b_router, w_gate_up, b_gate_up, w_down, b_down, final_norm_g):
    depth = w_in.shape[0]
    assert depth == 1
    b, s, _ = x_prompt.shape
    assert b == 1
    db, t_new, _ = x_sample.shape
    n_pool = cache_fox_k.shape[1]
    lam_init = 0.8 - 0.6 * math.exp(-0.3 * 0)

    w0 = w_in[0]
    g1 = norm1_g[0][None, :]
    w_c = jnp.concatenate([w0[:, FQ0:FF0], w0[:, DQ0:IN_WIDTH],
                           jnp.pad(w0[:, FF0:DQ0], ((0, 0), (0, LANES - FOX_HEADS)))], axis=1).astype(BF16)
    bf_pad = jnp.pad(b_fgate[0], (0, LANES - FOX_HEADS))[None, :]
    w_slots = jnp.stack([
        _pad_heads(w0[:, FQ0:FK0], 8, 64, SLOT), _pad_heads(w0[:, FK0:FV0], 8, 64, SLOT),
        _pad_heads(w0[:, FV0:FF0], 8, 64, SLOT), _pad_heads(w0[:, DQ0:DK0], 8, 64, SLOT),
        _pad_heads(w0[:, DK0:DV0], 8, 64, SLOT), _pad_heads(w0[:, DV0:IN_WIDTH], 4, 128, 2 * SLOT)]).astype(BF16)
    lam_vecs = jnp.stack([lambda_q1[0], lambda_k1[0], lambda_q2[0], lambda_k2[0]])
    subg = subln_g[0][None, :]

    xp = x_prompt.reshape(s, D_MODEL)
    tmp = _tile(s, (512, 256, 128))
    _, fk_p, fv_p, _, dk_p, dv_p, lf_p, _, cf_p = _proj(xp, g1, w_c, bf_pad, True, tmp)
    ops = _slots(xp, g1, cf_p, w_slots, tmp)
    mix_p = _attn(ops, _prompt_bias(rel_bias, tmp), lam_vecs, subg, tmp, lam_init)

    xs = x_sample.reshape(db * t_new, D_MODEL)
    tms = _tile(db * t_new, (512, 256, 128))
    fq_s, fk_s, fv_s, dq_s, dk_s, dv_s, lf_s, lfp_s, _ = _proj(xs, g1, w_c, bf_pad, False, tms)
    seq = lambda a: a.reshape(db, t_new, a.shape[-1])
    blast, bnew = _sample_bias(rel_bias, t_new)
    lfc = jnp.transpose(cache_fox_logf[0], (0, 2, 1))
    mix_s = _sample(page_table, seq(fq_s), seq(fk_s), seq(fv_s), seq(lfp_s), seq(dq_s), seq(dk_s), seq(dv_s),
                    blast, bnew, lam_vecs, subg,
                    cache_fox_k[0].reshape(n_pool, PAGE_SIZE, 512), cache_fox_v[0].reshape(n_pool, PAGE_SIZE, 512),
                    cache_diff_k[0].reshape(n_pool, PAGE_SIZE, 512), cache_diff_v[0].reshape(n_pool, PAGE_SIZE, 512),
                    lfc, _tile(page_table.shape[1], (8, 4, 2, 1)), lam_init)

    n_tok = s + db * t_new
    tmm = _tile(math.gcd(s, db * t_new), (1024, 512, 256, 128))
    wo = w_out[0]
    wo_slots = jnp.concatenate([_pad_heads(wo[:FOX_WIDTH].T, 8, 64, SLOT).T, wo[FOX_WIDTH:]], axis=0).astype(BF16)
    g2 = norm2_g[0][None, :]
    wr_pad = jnp.pad(w_router[0], ((0, 0), (0, LANES - N_EXPERTS)))
    br_pad = jnp.concatenate([b_router[0], jnp.full((LANES - N_EXPERTS,), NEG_INF, F32)])[None, :]
    post_p = _post(xp, mix_p, wo_slots, g2, wr_pad, br_pad, tmm)
    post_s = _post(xs, mix_s.reshape(db * t_new, D_MODEL), wo.astype(BF16), g2, wr_pad, br_pad, tmm)
    h_all, hn_all, slotc, gatec = (jnp.concatenate([a, c], axis=0) for a, c in zip(post_p[:4], post_s[:4]))
    slotr = jnp.concatenate([post_p[4], post_s[4]], axis=1)
    counts = jnp.concatenate([post_p[5], post_s[5]], axis=0)[:, 0, :N_EXPERTS].astype(jnp.int32).reshape(-1)
    cap = 160 if tmm == 1024 else (96 if tmm == 512 else 64)
    y_all = _moe(counts, hn_all, h_all, slotr, slotc, gatec,
                 w_gate_up[0].astype(BF16), b_gate_up[0][:, None, :], w_down[0].astype(BF16), b_down[0][:, None, :],
                 final_norm_g[None, :], tmm, cap)
    assert n_tok == y_all.shape[0]

    y_prompt = y_all[:s].reshape(b, s, D_MODEL)
    y_sample = y_all[s:].reshape(db, t_new, D_MODEL)
    return (y_prompt, y_sample,
            fk_p.reshape(1, b, s, 8, 64), fv_p.reshape(1, b, s, 8, 64), lf_p.reshape(1, b, s, 8),
            dk_p.reshape(1, b, s, 4, 128), dv_p.reshape(1, b, s, 4, 128),
            fk_s.reshape(1, db, t_new, 8, 64), fv_s.reshape(1, db, t_new, 8, 64), lf_s.reshape(1, db, t_new, 8),
            dk_s.reshape(1, db, t_new, 4, 128), dv_s.reshape(1, db, t_new, 4, 128))
```
